```python
import math
import jax, jax.numpy as jnp
from jax import lax
import numpy as np

D_MODEL = 1024
BATCH = 32
SEQ = 2048
DEPTH = 2

GRID_W = 64
CTX_LEN = 256
REC_W = 512
REC_BLOCKS = 8
REC_BLK = REC_W // REC_BLOCKS
REC_CONV = 4
LRU_C = 8.0
SC_W = 512
SC_CONV = 3
NA_HEADS = 16
NA_HEAD_DIM = 64
NA_W = NA_HEADS * NA_HEAD_DIM
NA_WIN_R = 8
NA_WIN_C = 16
D_FF = 2816
FFN_CONV = 3
N_EVEN = (DEPTH + 1) // 2
N_ODD = DEPTH // 2
EV_IN = 2 * REC_W + 3 * SC_W
ALPHA = (2.0 * DEPTH) ** 0.25
BETA = (8.0 * DEPTH) ** -0.25
LN_EPS = 1e-6
NEG_INF = -1e30

kernel_name = 'hybrid_rglru_shortconv_natten_convffn_deepnorm'


def layer_norm(x, g, b):
    xf = x.astype(jnp.float32)
    mu = jnp.mean(xf, axis=-1, keepdims=True)
    var = jnp.mean(jnp.square(xf - mu), axis=-1, keepdims=True)
    y = (xf - mu) * lax.rsqrt(var + LN_EPS)
    return (y * g.astype(jnp.float32) + b.astype(jnp.float32)).astype(x.dtype)


def modulate(x, shift, scale):
    return x * (1 + scale) + shift


def dwconv(x, w, b, pad_l, pad_r):
    out = lax.conv_general_dilated(
        x, w[:, None, :].astype(x.dtype), window_strides=(1,), padding=[(pad_l, pad_r)],
        dimension_numbers=('NWC', 'WIO', 'NWC'), feature_group_count=x.shape[-1])
    return out + b.astype(x.dtype)


def rglru_coeffs(xc, wa, ba, wx, bx, lam):
    B, L, _ = xc.shape
    xf = xc.astype(jnp.float32)
    xb = xf.reshape(B, L, REC_BLOCKS, REC_BLK)
    r = jax.nn.sigmoid(jnp.einsum('blni,nij->blnj', xb, wa.astype(jnp.float32)).reshape(B, L, REC_W) + ba)
    i = jax.nn.sigmoid(jnp.einsum('blni,nij->blnj', xb, wx.astype(jnp.float32)).reshape(B, L, REC_W) + bx)
    log_a = -LRU_C * r * jax.nn.softplus(-lam.astype(jnp.float32))
    a = jnp.exp(log_a)
    mult = jnp.sqrt(-jnp.expm1(2.0 * log_a))
    return a, mult * (i * xf)


def linear_scan(a, b, h0, reverse):
    if reverse:
        a, b = jnp.flip(a, 1), jnp.flip(b, 1)
    def combine(e1, e2):
        return e1[0] * e2[0], e2[0] * e1[1] + e2[1]
    a_cum, h = lax.associative_scan(combine, (a, b), axis=1)
    h = h + a_cum * h0[:, None, :]
    h_last = h[:, -1]
    if reverse:
        h = jnp.flip(h, 1)
    return h, h_last


def rec_branch(xr, conv_w, conv_b, wa, ba, wx, bx, lam, h0_f, h0_b):
    xc = dwconv(xr, conv_w, conv_b, REC_CONV // 2, REC_CONV - 1 - REC_CONV // 2)
    a_f, b_f = rglru_coeffs(xc, wa[0], ba[0], wx[0], bx[0], lam[0])
    a_b, b_b = rglru_coeffs(xc, wa[1], ba[1], wx[1], bx[1], lam[1])
    h_f, last_f = linear_scan(a_f, b_f, h0_f, False)
    h_b, last_b = linear_scan(a_b, b_b, h0_b, True)
    return (h_f + h_b).astype(xr.dtype), last_f, last_b


def even_mixer(hx, hc, w_in, w_out, rc_w, rc_b, wa, ba, wx, bx, lam, sc_w, sc_b, ctx_out):
    cuts = [REC_W, 2 * REC_W, 2 * REC_W + SC_W, 2 * REC_W + 2 * SC_W]
    xr_x, gr_x, sb_x, sg_x, sx_x = jnp.split(hx @ w_in, cuts, axis=-1)
    xr_c, gr_c, sb_c, sg_c, sx_c = jnp.split(hc @ w_in, cuts, axis=-1)
    B = hc.shape[0]
    zeros = jnp.zeros((B, REC_W), jnp.float32)
    rec_c, last_f, last_b = rec_branch(xr_c, rc_w, rc_b, wa, ba, wx, bx, lam, zeros, zeros)
    rec_x, _, _ = rec_branch(xr_x, rc_w, rc_b, wa, ba, wx, bx, lam, last_f, last_b)

    def merge(rec, gr, sb, sg, sx):
        y_rec = rec * jax.nn.gelu(gr)
        y_sc = sb * dwconv(sg * sx, sc_w, sc_b, SC_CONV // 2, SC_CONV // 2)
        return jnp.concatenate([y_rec, y_sc], axis=-1) @ w_out

    y_x = merge(rec_x, gr_x, sb_x, sg_x, sx_x)
    y_c = merge(rec_c, gr_c, sb_c, sg_c, sx_c) if ctx_out else None
    return y_x, y_c


def odd_mixer(hx, hc, w_qkv, w_out, rpb, ctx_out):
    B, S, _ = hx.shape
    L_c = hc.shape[1]
    rows = S // GRID_W
    win_r = min(NA_WIN_R, rows)
    scale = NA_HEAD_DIM ** -0.5
    qx, kx, vx = [t.reshape(B, S, NA_HEADS, NA_HEAD_DIM) for t in jnp.split(hx @ w_qkv, 3, axis=-1)]
    qc, kc, vc = [t.reshape(B, L_c, NA_HEADS, NA_HEAD_DIM) for t in jnp.split(hc @ w_qkv, 3, axis=-1)]
    kc_h = kc.transpose(0, 2, 1, 3)
    vc_h = vc.transpose(0, 2, 1, 3)
    k_g = kx.reshape(B, rows, GRID_W, NA_HEADS, NA_HEAD_DIM).transpose(0, 3, 1, 2, 4)
    v_g = vx.reshape(B, rows, GRID_W, NA_HEADS, NA_HEAD_DIM).transpose(0, 3, 1, 2, 4)
    q_rows = qx.reshape(B, rows, GRID_W, NA_HEADS, NA_HEAD_DIM).transpose(1, 0, 3, 2, 4)
    cols = jnp.arange(GRID_W)
    col_start = jnp.clip(cols - NA_WIN_C // 2, 0, GRID_W - NA_WIN_C)
    in_win = (cols[None, :] >= col_start[:, None]) & (cols[None, :] < col_start[:, None] + NA_WIN_C)
    dc = jnp.clip(cols[None, :] - cols[:, None], -(NA_WIN_C - 1), NA_WIN_C - 1) + NA_WIN_C - 1
    rpb32 = rpb.astype(jnp.float32)

    def row_block(args):
        q_r, r = args
        rs = jnp.clip(r - win_r // 2, 0, rows - win_r)
        k_b = lax.dynamic_slice_in_dim(k_g, rs, win_r, axis=2)
        v_b = lax.dynamic_slice_in_dim(v_g, rs, win_r, axis=2)
        dr = rs + jnp.arange(win_r) - r + NA_WIN_R - 1
        bias = rpb32[:, dr[None, :, None], dc[:, None, :]]
        s_loc = jnp.einsum('bhqd,bhrkd->bhqrk', q_r, k_b).astype(jnp.float32) * scale + bias[None]
        s_loc = jnp.where(in_win[:, None, :], s_loc, NEG_INF)
        s_ctx = jnp.einsum('bhqd,bhkd->bhqk', q_r, kc_h).astype(jnp.float32) * scale
        s_all = jnp.concatenate([s_loc.reshape(B, NA_HEADS, GRID_W, win_r * GRID_W), s_ctx], axis=-1)
        p = jax.nn.softmax(s_all, axis=-1).astype(v_b.dtype)
        p_loc = p[..., :win_r * GRID_W].reshape(B, NA_HEADS, GRID_W, win_r, GRID_W)
        p_ctx = p[..., win_r * GRID_W:]
        return jnp.einsum('bhqrk,bhrkd->bhqd', p_loc, v_b) + jnp.einsum('bhqk,bhkd->bhqd', p_ctx, vc_h)

    o = lax.map(row_block, (q_rows, jnp.arange(rows)))
    y_x = o.transpose(1, 0, 3, 2, 4).reshape(B, S, NA_W) @ w_out
    y_c = None
    if ctx_out:
        s = jnp.einsum('bqhd,bkhd->bhqk', qc, kc).astype(jnp.float32) * scale
        p = jax.nn.softmax(s, axis=-1).astype(vc.dtype)
        y_c = jnp.einsum('bhqk,bkhd->bqhd', p, vc).reshape(B, L_c, NA_W) @ w_out
    return y_x, y_c


def conv_ffn(h, w_gate, w_up, conv_w, conv_b, w_down):
    g = dwconv(h @ w_gate, conv_w, conv_b, FFN_CONV // 2, FFN_CONV // 2)
    return (jax.nn.silu(g) * (h @ w_up)) @ w_down


def setup_inputs(seed: int = 0) -> dict:
    key = jax.random.key(seed)
    ks = iter(jax.random.split(key, 40))
    f32 = jnp.float32

    def nrm(shape, scale):
        return jax.random.normal(next(ks), shape, f32) * scale

    u = jax.random.uniform(next(ks), (N_EVEN, 2, REC_W), f32, minval=0.9, maxval=0.999)
    a0 = u ** (1.0 / LRU_C)
    lam = jnp.log(a0) - jnp.log1p(-a0)
    return {
        'x': nrm((BATCH, SEQ, D_MODEL), 1.0),
        'c': nrm((BATCH, D_MODEL), 1.0),
        'ctx': nrm((BATCH, CTX_LEN, D_MODEL), 1.0),
        'c_ctx': nrm((D_MODEL,), 1.0),
        'mod_w': nrm((DEPTH, D_MODEL, 6 * D_MODEL), D_MODEL ** -0.5),
        'mod_b': nrm((DEPTH, 6 * D_MODEL), 0.01),
        'ln1_g': 1.0 + nrm((DEPTH, D_MODEL), 0.01),
        'ln1_b': nrm((DEPTH, D_MODEL), 0.01),
        'ln2_g': 1.0 + nrm((DEPTH, D_MODEL), 0.01),
        'ln2_b': nrm((DEPTH, D_MODEL), 0.01),
        'ev_w_in': nrm((N_EVEN, D_MODEL, EV_IN), D_MODEL ** -0.5),
        'ev_w_out': nrm((N_EVEN, REC_W + SC_W, D_MODEL), BETA * (REC_W + SC_W) ** -0.5),
        'rec_conv_w': nrm((N_EVEN, REC_CONV, REC_W), REC_CONV ** -0.5),
        'rec_conv_b': nrm((N_EVEN, REC_W), 0.01),
        'rec_wa': nrm((N_EVEN, 2, REC_BLOCKS, REC_BLK, REC_BLK), REC_BLK ** -0.5),
        'rec_ba': nrm((N_EVEN, 2, REC_W), 0.01),
        'rec_wx': nrm((N_EVEN, 2, REC_BLOCKS, REC_BLK, REC_BLK), REC_BLK ** -0.5),
        'rec_bx': nrm((N_EVEN, 2, REC_W), 0.01),
        'rec_lam': lam,
        'sc_conv_w': nrm((N_EVEN, SC_CONV, SC_W), SC_CONV ** -0.5),
        'sc_conv_b': nrm((N_EVEN, SC_W), 0.01),
        'na_w_qkv': nrm((N_ODD, D_MODEL, 3 * NA_W), D_MODEL ** -0.5),
        'na_w_out': nrm((N_ODD, NA_W, D_MODEL), BETA * NA_W ** -0.5),
        'na_rpb': nrm((N_ODD, NA_HEADS, 2 * NA_WIN_R - 1, 2 * NA_WIN_C - 1), 0.1),
        'ffn_w_gate': nrm((DEPTH, D_MODEL, D_FF), D_MODEL ** -0.5),
        'ffn_w_up': nrm((DEPTH, D_MODEL, D_FF), D_MODEL ** -0.5),
        'ffn_conv_w': nrm((DEPTH, FFN_CONV, D_FF), FFN_CONV ** -0.5),
        'ffn_conv_b': nrm((DEPTH, D_FF), 0.01),
        'ffn_w_down': nrm((DEPTH, D_FF, D_MODEL), BETA * D_FF ** -0.5),
    }


def reference(x, c, ctx, c_ctx, mod_w, mod_b, ln1_g, ln1_b, ln2_g, ln2_b, ev_w_in, ev_w_out,
              rec_conv_w, rec_conv_b, rec_wa, rec_ba, rec_wx, rec_bx, rec_lam, sc_conv_w, sc_conv_b,
              na_w_qkv, na_w_out, na_rpb, ffn_w_gate, ffn_w_up, ffn_conv_w, ffn_conv_b, ffn_w_down):
    sc_c = jax.nn.silu(c)
    sc_ctx = jax.nn.silu(c_ctx)[None, :]
    for i in range(DEPTH):
        last = i == DEPTH - 1
        j = i // 2
        m_x = [t[:, None, :] for t in jnp.split(sc_c @ mod_w[i] + mod_b[i], 6, axis=-1)]
        m_c = [t[:, None, :] for t in jnp.split(sc_ctx @ mod_w[i] + mod_b[i], 6, axis=-1)]
        hx = modulate(x, m_x[0], m_x[1])
        hc = modulate(ctx, m_c[0], m_c[1])
        if i % 2 == 0:
            y_x, y_c = even_mixer(hx, hc, ev_w_in[j], ev_w_out[j], rec_conv_w[j], rec_conv_b[j],
                                  rec_wa[j], rec_ba[j], rec_wx[j], rec_bx[j], rec_lam[j],
                                  sc_conv_w[j], sc_conv_b[j], not last)
        else:
            y_x, y_c = odd_mixer(hx, hc, na_w_qkv[j], na_w_out[j], na_rpb[j], not last)
        x = layer_norm(ALPHA * x + m_x[2] * y_x, ln1_g[i], ln1_b[i])
        f_x = conv_ffn(modulate(x, m_x[3], m_x[4]), ffn_w_gate[i], ffn_w_up[i], ffn_conv_w[i], ffn_conv_b[i], ffn_w_down[i])
        x = layer_norm(ALPHA * x + m_x[5] * f_x, ln2_g[i], ln2_b[i])
        if not last:
            ctx = layer_norm(ALPHA * ctx + m_c[2] * y_c, ln1_g[i], ln1_b[i])
            f_c = conv_ffn(modulate(ctx, m_c[3], m_c[4]), ffn_w_gate[i], ffn_w_up[i], ffn_conv_w[i], ffn_conv_b[i], ffn_w_down[i])
            ctx = layer_norm(ALPHA * ctx + m_c[5] * f_c, ln2_g[i], ln2_b[i])
    return x
```

```python
import functools

import jax
import jax.numpy as jnp
from jax import lax
from jax.experimental import pallas as pl
from jax.experimental.pallas import tpu as pltpu

D_MODEL = 1024
DEPTH = 2
GRID_W = 64
REC_W = 512
REC_BLOCKS = 8
REC_BLK = REC_W // REC_BLOCKS
REC_CONV = 4
LRU_C = 8.0
SC_W = 512
SC_CONV = 3
NA_HEADS = 16
NA_HEAD_DIM = 64
NA_W = NA_HEADS * NA_HEAD_DIM
NA_WIN_R = 8
NA_WIN_C = 16
D_FF = 2816
FFN_CONV = 3
EV_IN = 2 * REC_W + 3 * SC_W
EV_MID = 5 * REC_W
ALPHA = (2.0 * DEPTH) ** 0.25
LN_EPS = 1e-6
NEG_INF = -1e30

SUBLANES = 8
LANES = 128
HALO = SUBLANES
VMEM_LIMIT = 56 * 1024 * 1024

F32 = jnp.float32
BF16 = jnp.bfloat16


def _params(sem):
    return pltpu.CompilerParams(dimension_semantics=sem, vmem_limit_bytes=VMEM_LIMIT)


def _const_spec(shape):
    nd = len(shape)
    return pl.BlockSpec(shape, lambda *_: (0,) * nd)


def _layer_norm(v, g, b):
    mu = jnp.mean(v, axis=-1, keepdims=True)
    d = v - mu
    var = jnp.mean(d * d, axis=-1, keepdims=True)
    return d * lax.rsqrt(var + LN_EPS) * g + b


def _shift_rows(v, k):
    n = v.shape[0]
    return pltpu.roll(v, (-k) % n, 0)


def _modulated_with_halo(prev_ref, main_ref, next_ref, shift, scale, nt):
    i = pl.program_id(1)
    keep_prev = (i > 0).astype(F32)
    keep_next = (i < nt - 1).astype(F32)
    hp = (prev_ref[0] * (1.0 + scale) + shift) * keep_prev
    hm = main_ref[0] * (1.0 + scale) + shift
    hn = (next_ref[0] * (1.0 + scale) + shift) * keep_next
    return jnp.concatenate([hp, hm, hn], axis=0)


def _halo_specs(tm, nt, width, order):
    nb = tm // HALO
    last = nt * nb - 1
    main = pl.BlockSpec((1, tm, width), lambda b, i: (b, order(i), 0))
    prev = pl.BlockSpec((1, HALO, width), lambda b, i: (b, jnp.maximum(order(i) * nb - 1, 0), 0))
    nxt = pl.BlockSpec((1, HALO, width), lambda b, i: (b, jnp.minimum((order(i) + 1) * nb, last), 0))
    return prev, main, nxt


def _mod_spec(mod):
    if mod.shape[0] == 1:
        return pl.BlockSpec((1, SUBLANES, D_MODEL), lambda b, i: (0, 0, 0))
    return pl.BlockSpec((1, SUBLANES, D_MODEL), lambda b, i: (b, 0, 0))


def _mod_kernel(c_ref, w_ref, b_ref, o_ref):
    c = c_ref[...]
    sc = c * jax.nn.sigmoid(c)
    o_ref[0] = jnp.dot(sc, w_ref[0], preferred_element_type=F32) + b_ref[0]


def _modulation(c_all, mod_w, mod_b):
    rows = c_all.shape[0]
    tn = 1024
    return pl.pallas_call(
        _mod_kernel,
        grid=(DEPTH, 6 * D_MODEL // tn),
        in_specs=[
            pl.BlockSpec((rows, D_MODEL), lambda l, j: (0, 0)),
            pl.BlockSpec((1, D_MODEL, tn), lambda l, j: (l, 0, j)),
            pl.BlockSpec((1, 1, tn), lambda l, j: (l, 0, j)),
        ],
        out_specs=pl.BlockSpec((1, rows, tn), lambda l, j: (l, 0, j)),
        out_shape=jax.ShapeDtypeStruct((DEPTH, rows, 6 * D_MODEL), F32),
        compiler_params=_params(("arbitrary", "arbitrary")),
        name="modulation",
    )(c_all, mod_w, mod_b.reshape(DEPTH, 1, 6 * D_MODEL))


def _scan_tile(a, b, h0, reverse):
    T = a.shape[0]
    row = lax.broadcasted_iota(jnp.int32, a.shape, 0) & (SUBLANES - 1)
    d = 1
    while d < SUBLANES:
        if reverse:
            a_s, b_s = _shift_rows(a, d), _shift_rows(b, d)
            valid = row < SUBLANES - d
        else:
            a_s, b_s = _shift_rows(a, -d), _shift_rows(b, -d)
            valid = row >= d
        b = jnp.where(valid, a * b_s + b, b)
        a = jnp.where(valid, a * a_s, a)
        d *= 2
    nblk = T // SUBLANES
    out = [None] * nblk
    h = h0
    order = range(nblk - 1, -1, -1) if reverse else range(nblk)
    for j in order:
        sl = slice(j * SUBLANES, (j + 1) * SUBLANES)
        hb = b[sl] + a[sl] * h
        out[j] = hb
        h = hb[0:1] if reverse else hb[SUBLANES - 1:SUBLANES]
    return jnp.concatenate(out, axis=0), h


def _even_fwd_kernel(prev_ref, main_ref, next_ref, mod_ref, w_in_ref, rcw_ref, rcb_ref, wg_ref, bg_ref,
                     lam_ref, scw_ref, scb_ref, h0_ref, mid_ref, hlast_ref, h_sc, *, tm, nt):
    i = pl.program_id(1)

    @pl.when(i == 0)
    def _():
        h_sc[...] = h0_ref[0]

    shift, scale = mod_ref[0, 0:1, :], mod_ref[0, 1:2, :]
    hm = _modulated_with_halo(prev_ref, main_ref, next_ref, shift, scale, nt)
    proj = jnp.dot(hm.astype(BF16), w_in_ref[...], preferred_element_type=F32)
    xr = proj[:, 0:REC_W]
    gr = proj[HALO:HALO + tm, REC_W:2 * REC_W]
    sb = proj[HALO:HALO + tm, 2 * REC_W:2 * REC_W + SC_W]
    sg = proj[:, 2 * REC_W + SC_W:2 * REC_W + 2 * SC_W]
    sx = proj[:, 2 * REC_W + 2 * SC_W:]

    xc = rcb_ref[...] + sum(rcw_ref[k:k + 1, :] * _shift_rows(xr, k - REC_CONV // 2) for k in range(REC_CONV))
    xc = xc[HALO:HALO + tm]

    gates = jnp.dot(xc.astype(BF16), wg_ref[...], preferred_element_type=F32) + bg_ref[...]
    neg_c_sp = -LRU_C * jax.nn.softplus(-lam_ref[...])
    coeffs = []
    for dirn in range(2):
        r = jax.nn.sigmoid(gates[:, (2 * dirn) * REC_W:(2 * dirn + 1) * REC_W])
        g_in = jax.nn.sigmoid(gates[:, (2 * dirn + 1) * REC_W:(2 * dirn + 2) * REC_W])
        log_a = r * neg_c_sp[:, dirn * REC_W:(dirn + 1) * REC_W]
        a = jnp.exp(log_a)
        th = jnp.tanh(log_a)
        mult = jnp.sqrt(-2.0 * th / (1.0 - th))
        coeffs.append((a, mult * (g_in * xc)))

    h_f, h_carry = _scan_tile(coeffs[0][0], coeffs[0][1], h_sc[...], reverse=False)
    h_sc[...] = h_carry
    hlast_ref[0] = h_carry

    s = sg * sx
    sconv = scb_ref[...] + sum(scw_ref[k:k + 1, :] * _shift_rows(s, k - SC_CONV // 2) for k in range(SC_CONV))
    y_sc = sb * sconv[HALO:HALO + tm]

    mid_ref[0, :, 0:REC_W] = h_f
    mid_ref[0, :, REC_W:2 * REC_W] = coeffs[1][0]
    mid_ref[0, :, 2 * REC_W:3 * REC_W] = coeffs[1][1]
    mid_ref[0, :, 3 * REC_W:4 * REC_W] = jax.nn.gelu(gr)
    mid_ref[0, :, 4 * REC_W:5 * REC_W] = y_sc


def _even_fwd(x, mod, w_in, rcw, rcb, wg, bg, lam, scw, scb, h0, tm):
    B, L, _ = x.shape
    nt = L // tm
    prev, main, nxt = _halo_specs(tm, nt, D_MODEL, lambda i: i)
    return pl.pallas_call(
        functools.partial(_even_fwd_kernel, tm=tm, nt=nt),
        grid=(B, nt),
        in_specs=[prev, main, nxt, _mod_spec(mod), _const_spec(w_in.shape), _const_spec(rcw.shape),
                  _const_spec(rcb.shape), _const_spec(wg.shape), _const_spec(bg.shape), _const_spec(lam.shape),
                  _const_spec(scw.shape), _const_spec(scb.shape),
                  pl.BlockSpec((1, 1, REC_W), lambda b, i: (b, 0, 0))],
        out_specs=[pl.BlockSpec((1, tm, EV_MID), lambda b, i: (b, i, 0)),
                   pl.BlockSpec((1, 1, REC_W), lambda b, i: (b, 0, 0))],
        out_shape=[jax.ShapeDtypeStruct((B, L, EV_MID), F32), jax.ShapeDtypeStruct((B, 1, REC_W), F32)],
        scratch_shapes=[pltpu.VMEM((1, REC_W), F32)],
        compiler_params=_params(("arbitrary", "arbitrary")),
        name="even_fwd",
    )(x, x, x, mod, w_in, rcw, rcb, wg, bg, lam, scw, scb, h0)


def _even_bwd_kernel(mid_ref, x_ref, mod_ref, w_out_ref, g_ref, b_ref, h0_ref, o_ref, hlast_ref, h_sc):
    i = pl.program_id(1)

    @pl.when(i == 0)
    def _():
        h_sc[...] = h0_ref[0]

    h_f = mid_ref[0, :, 0:REC_W]
    a_b = mid_ref[0, :, REC_W:2 * REC_W]
    b_b = mid_ref[0, :, 2 * REC_W:3 * REC_W]
    gg = mid_ref[0, :, 3 * REC_W:4 * REC_W]
    y_sc = mid_ref[0, :, 4 * REC_W:5 * REC_W]

    h_b, h_carry = _scan_tile(a_b, b_b, h_sc[...], reverse=True)
    h_sc[...] = h_carry
    hlast_ref[0] = h_carry

    y_rec = (h_f + h_b) * gg
    y = (jnp.dot(y_rec.astype(BF16), w_out_ref[0:REC_W, :], preferred_element_type=F32)
         + jnp.dot(y_sc.astype(BF16), w_out_ref[REC_W:, :], preferred_element_type=F32))
    gate = mod_ref[0, 2:3, :]
    o_ref[0] = _layer_norm(ALPHA * x_ref[0] + gate * y, g_ref[...], b_ref[...])


def _even_bwd(mid, x, mod, w_out, ln_g, ln_b, h0, tm):
    B, L, _ = x.shape
    nt = L // tm
    rev = lambda i: nt - 1 - i
    return pl.pallas_call(
        _even_bwd_kernel,
        grid=(B, nt),
        in_specs=[pl.BlockSpec((1, tm, EV_MID), lambda b, i: (b, rev(i), 0)),
                  pl.BlockSpec((1, tm, D_MODEL), lambda b, i: (b, rev(i), 0)),
                  _mod_spec(mod), _const_spec(w_out.shape), _const_spec(ln_g.shape), _const_spec(ln_b.shape),
                  pl.BlockSpec((1, 1, REC_W), lambda b, i: (b, 0, 0))],
        out_specs=[pl.BlockSpec((1, tm, D_MODEL), lambda b, i: (b, rev(i), 0)),
                   pl.BlockSpec((1, 1, REC_W), lambda b, i: (b, 0, 0))],
        out_shape=[jax.ShapeDtypeStruct((B, L, D_MODEL), F32), jax.ShapeDtypeStruct((B, 1, REC_W), F32)],
        scratch_shapes=[pltpu.VMEM((1, REC_W), F32)],
        compiler_params=_params(("arbitrary", "arbitrary")),
        name="even_bwd",
    )(mid, x, mod, w_out, ln_g, ln_b, h0)


def _ffn_kernel(prev_ref, main_ref, next_ref, mod_ref, wg_ref, wu_ref, cw_ref, cb_ref, wd_ref, g_ref, b_ref,
                o_ref, *, tm, nt):
    shift, scale, gate = mod_ref[0, 3:4, :], mod_ref[0, 4:5, :], mod_ref[0, 5:6, :]
    hm = _modulated_with_halo(prev_ref, main_ref, next_ref, shift, scale, nt).astype(BF16)
    gpre = jnp.dot(hm, wg_ref[...], preferred_element_type=F32)
    conv = cb_ref[...] + sum(cw_ref[k:k + 1, :] * _shift_rows(gpre, k - FFN_CONV // 2) for k in range(FFN_CONV))
    conv = conv[HALO:HALO + tm]
    up = jnp.dot(hm[HALO:HALO + tm], wu_ref[...], preferred_element_type=F32)
    act = (conv * jax.nn.sigmoid(conv)) * up
    f = jnp.dot(act.astype(BF16), wd_ref[...], preferred_element_type=F32)
    o_ref[0] = _layer_norm(ALPHA * main_ref[0] + gate * f, g_ref[...], b_ref[...])


def _ffn(x, mod, w_gate, w_up, conv_w, conv_b, w_down, ln_g, ln_b, tm):
    B, L, _ = x.shape
    nt = L // tm
    prev, main, nxt = _halo_specs(tm, nt, D_MODEL, lambda i: i)
    return pl.pallas_call(
        functools.partial(_ffn_kernel, tm=tm, nt=nt),
        grid=(B, nt),
        in_specs=[prev, main, nxt, _mod_spec(mod), _const_spec(w_gate.shape), _const_spec(w_up.shape),
                  _const_spec(conv_w.shape), _const_spec(conv_b.shape), _const_spec(w_down.shape),
                  _const_spec(ln_g.shape), _const_spec(ln_b.shape)],
        out_specs=pl.BlockSpec((1, tm, D_MODEL), lambda b, i: (b, i, 0)),
        out_shape=jax.ShapeDtypeStruct((B, L, D_MODEL), F32),
        compiler_params=_params(("arbitrary", "arbitrary")),
        name="conv_ffn",
    )(x, x, x, mod, w_gate, w_up, conv_w, conv_b, w_down, ln_g, ln_b)


def _modproj_kernel(x_ref, mod_ref, w_ref, o_ref):
    shift, scale = mod_ref[0, 0:1, :], mod_ref[0, 1:2, :]
    h = x_ref[0] * (1.0 + scale) + shift
    o_ref[0] = jnp.dot(h.astype(BF16), w_ref[...], preferred_element_type=F32).astype(o_ref.dtype)


def _modproj(x, mod, w, tm):
    B, L, _ = x.shape
    n = w.shape[1]
    return pl.pallas_call(
        _modproj_kernel,
        grid=(B, L // tm),
        in_specs=[pl.BlockSpec((1, tm, D_MODEL), lambda b, i: (b, i, 0)), _mod_spec(mod), _const_spec(w.shape)],
        out_specs=pl.BlockSpec((1, tm, n), lambda b, i: (b, i, 0)),
        out_shape=jax.ShapeDtypeStruct((B, L, n), BF16),
        compiler_params=_params(("arbitrary", "arbitrary")),
        name="mod_proj",
    )(x, mod, w)


def _attn_kernel(q_ref, k_ref, v_ref, kc_ref, vc_ref, bias_ref, o_ref, *, rows):
    win_keys = NA_WIN_R * GRID_W
    lane = lax.broadcasted_iota(jnp.int32, (GRID_W, LANES), 1)
    first_head = lane < NA_HEAD_DIM
    scale = NA_HEAD_DIM ** -0.5
    kc = kc_ref[0]
    vc = vc_ref[0]
    dims = (((1,), (1,)), ((), ()))

    def row_body(r, carry):
        rs = jnp.clip(r - NA_WIN_R // 2, 0, rows - NA_WIN_R)
        cls = rs - r + NA_WIN_R - 1
        q = q_ref[0, pl.ds(pl.multiple_of(r * GRID_W, GRID_W), GRID_W), :] * scale
        k_loc = k_ref[0, pl.ds(pl.multiple_of(rs * GRID_W, GRID_W), win_keys), :]
        v_loc = v_ref[0, pl.ds(pl.multiple_of(rs * GRID_W, GRID_W), win_keys), :]
        outs = []
        for hh in range(2):
            keep = first_head if hh == 0 else jnp.logical_not(first_head)
            qh = jnp.where(keep, q, jnp.zeros_like(q))
            s_loc = lax.dot_general(qh, k_loc, dims, preferred_element_type=F32) + bias_ref[hh, cls]
            s_ctx = lax.dot_general(qh, kc, dims, preferred_element_type=F32)
            m = jnp.maximum(jnp.max(s_loc, axis=-1, keepdims=True), jnp.max(s_ctx, axis=-1, keepdims=True))
            p_loc = jnp.exp(s_loc - m)
            p_ctx = jnp.exp(s_ctx - m)
            denom = jnp.sum(p_loc, axis=-1, keepdims=True) + jnp.sum(p_ctx, axis=-1, keepdims=True)
            o = (jnp.dot(p_loc.astype(BF16), v_loc, preferred_element_type=F32)
                 + jnp.dot(p_ctx.astype(BF16), vc, preferred_element_type=F32))
            outs.append(o / denom)
        o_ref[0, pl.ds(pl.multiple_of(r * GRID_W, GRID_W), GRID_W), :] = jnp.where(
            first_head, outs[0], outs[1]).astype(o_ref.dtype)
        return carry

    lax.fori_loop(0, rows, row_body, 0)


def _attention(qkv, kvc, bias):
    B, S, _ = qkv.shape
    Lc = kvc.shape[1]
    rows = S // GRID_W
    npairs = NA_W // LANES
    return pl.pallas_call(
        functools.partial(_attn_kernel, rows=rows),
        grid=(npairs, B),
        in_specs=[pl.BlockSpec((1, S, LANES), lambda p, b: (b, 0, p)),
                  pl.BlockSpec((1, S, LANES), lambda p, b: (b, 0, npairs + p)),
                  pl.BlockSpec((1, S, LANES), lambda p, b: (b, 0, 2 * npairs + p)),
                  pl.BlockSpec((1, Lc, LANES), lambda p, b: (b, 0, p)),
                  pl.BlockSpec((1, Lc, LANES), lambda p, b: (b, 0, npairs + p)),
                  pl.BlockSpec((2, NA_WIN_R, GRID_W, NA_WIN_R * GRID_W), lambda p, b: (p, 0, 0, 0))],
        out_specs=pl.BlockSpec((1, S, LANES), lambda p, b: (b, 0, p)),
        out_shape=jax.ShapeDtypeStruct((B, S, NA_W), BF16),
        compiler_params=_params(("arbitrary", "arbitrary")),
        name="nbr_attention",
    )(qkv, qkv, qkv, kvc, kvc, bias)


def _attn_bias_table(rpb):
    cols = jnp.arange(GRID_W)
    col_start = jnp.clip(cols - NA_WIN_C // 2, 0, GRID_W - NA_WIN_C)
    in_win = (cols[None, :] >= col_start[:, None]) & (cols[None, :] < col_start[:, None] + NA_WIN_C)
    dc = jnp.clip(cols[None, :] - cols[:, None], -(NA_WIN_C - 1), NA_WIN_C - 1) + NA_WIN_C - 1
    dr = jnp.arange(NA_WIN_R)[:, None] + jnp.arange(NA_WIN_R)[None, :]
    tbl = rpb.astype(F32)[:, dr[:, None, :, None], dc[None, :, None, :]]
    tbl = jnp.where(in_win[None, None, :, None, :], tbl, NEG_INF)
    return tbl.reshape(NA_HEADS, NA_WIN_R, GRID_W, NA_WIN_R * GRID_W)


def _outproj_kernel(a_ref, x_ref, mod_ref, w_ref, g_ref, b_ref, o_ref):
    y = jnp.dot(a_ref[0], w_ref[...], preferred_element_type=F32)
    gate = mod_ref[0, 2:3, :]
    o_ref[0] = _layer_norm(ALPHA * x_ref[0] + gate * y, g_ref[...], b_ref[...])


def _outproj(a, x, mod, w, ln_g, ln_b, tm):
    B, L, _ = x.shape
    return pl.pallas_call(
        _outproj_kernel,
        grid=(B, L // tm),
        in_specs=[pl.BlockSpec((1, tm, NA_W), lambda b, i: (b, i, 0)),
                  pl.BlockSpec((1, tm, D_MODEL), lambda b, i: (b, i, 0)),
                  _mod_spec(mod), _const_spec(w.shape), _const_spec(ln_g.shape), _const_spec(ln_b.shape)],
        out_specs=pl.BlockSpec((1, tm, D_MODEL), lambda b, i: (b, i, 0)),
        out_shape=jax.ShapeDtypeStruct((B, L, D_MODEL), F32),
        compiler_params=_params(("arbitrary", "arbitrary")),
        name="attn_out_proj",
    )(a, x, mod, w, ln_g, ln_b)


def _block_diag(w):
    eye = jnp.eye(REC_BLOCKS, dtype=w.dtype)
    return (w[:, :, None, :] * eye[:, None, :, None]).reshape(REC_W, REC_W)


def _pad_rows(w, rows=SUBLANES):
    return jnp.pad(w, ((0, rows - w.shape[0]), (0, 0)))


def _mod_rows(m):
    m = m.reshape(m.shape[:-1] + (6, D_MODEL))
    return jnp.pad(m, [(0, 0)] * (m.ndim - 2) + [(0, 2), (0, 0)])


def kernel(x, c, ctx, c_ctx, mod_w, mod_b, ln1_g, ln1_b, ln2_g, ln2_b, ev_w_in, ev_w_out, rec_conv_w, rec_conv_b,
           rec_wa, rec_ba, rec_wx, rec_bx, rec_lam, sc_conv_w, sc_conv_b, na_w_qkv, na_w_out, na_rpb, ffn_w_gate,
           ffn_w_up, ffn_conv_w, ffn_conv_b, ffn_w_down):
    B = x.shape[0]
    tm_x, tm_c = 256, 256

    c_all = jnp.concatenate([c, c_ctx[None, :], jnp.zeros((SUBLANES - 1, D_MODEL), F32)], axis=0)
    mods = _modulation(c_all, mod_w, mod_b)
    mod_x = [_mod_rows(mods[i, :B]) for i in range(DEPTH)]
    mod_c = [_mod_rows(mods[i, B:B + 1]) for i in range(DEPTH)]

    row = lambda v: v.reshape(1, -1).astype(F32)
    ffn = lambda i: (ffn_w_gate[i].astype(BF16), ffn_w_up[i].astype(BF16), _pad_rows(ffn_conv_w[i]),
                     row(ffn_conv_b[i]), ffn_w_down[i].astype(BF16), row(ln2_g[i]), row(ln2_b[i]))

    w_in = ev_w_in[0].astype(BF16)
    w_out = ev_w_out[0].astype(BF16)
    wg = jnp.concatenate([_block_diag(rec_wa[0, 0]), _block_diag(rec_wx[0, 0]),
                          _block_diag(rec_wa[0, 1]), _block_diag(rec_wx[0, 1])], axis=1).astype(BF16)
    bg = jnp.concatenate([rec_ba[0, 0], rec_bx[0, 0], rec_ba[0, 1], rec_bx[0, 1]]).reshape(1, -1)
    lam = rec_lam[0].reshape(1, 2 * REC_W)
    rec_args = (w_in, _pad_rows(rec_conv_w[0]), row(rec_conv_b[0]), wg, bg, lam, _pad_rows(sc_conv_w[0]),
                row(sc_conv_b[0]))
    zeros = jnp.zeros((B, 1, REC_W), F32)

    mid_c, last_f = _even_fwd(ctx, mod_c[0], *rec_args, zeros, tm_c)
    mid_x, _ = _even_fwd(x, mod_x[0], *rec_args, last_f, tm_x)
    ctx, last_b = _even_bwd(mid_c, ctx, mod_c[0], w_out, row(ln1_g[0]), row(ln1_b[0]), zeros, tm_c)
    x, _ = _even_bwd(mid_x, x, mod_x[0], w_out, row(ln1_g[0]), row(ln1_b[0]), last_b, tm_x)
    ctx = _ffn(ctx, mod_c[0], *ffn(0), tm_c)
    x = _ffn(x, mod_x[0], *ffn(0), tm_x)

    w_qkv = na_w_qkv[0].astype(BF16)
    qkv = _modproj(x, mod_x[1], w_qkv, tm_x)
    kvc = _modproj(ctx, mod_c[1], w_qkv[:, NA_W:], tm_c)
    att = _attention(qkv, kvc, _attn_bias_table(na_rpb[0]))
    x = _outproj(att, x, mod_x[1], na_w_out[0].astype(BF16), row(ln1_g[1]), row(ln1_b[1]), tm_x)
    x = _ffn(x, mod_x[1], *ffn(1), tm_x)
    return x
```

```python
import functools

import jax
import jax.numpy as jnp
from jax import lax
from jax.experimental import pallas as pl
from jax.experimental.pallas import tpu as pltpu

D_MODEL = 1024
DEPTH = 2
GRID_W = 64
REC_W = 512
REC_BLOCKS = 8
REC_BLK = REC_W // REC_BLOCKS
REC_CONV = 4
LRU_C = 8.0
SC_W = 512
SC_CONV = 3
NA_HEADS = 16
NA_HEAD_DIM = 64
NA_W = NA_HEADS * NA_HEAD_DIM
NA_WIN_R = 8
NA_WIN_C = 16
D_FF = 2816
FFN_CONV = 3
EV_IN = 2 * REC_W + 3 * SC_W
EV_MID = 5 * REC_W
ALPHA = (2.0 * DEPTH) ** 0.25
LN_EPS = 1e-6
NEG_INF = -1e30

SUBLANES = 8
LANES = 128
HALO = SUBLANES
VMEM_LIMIT = 56 * 1024 * 1024

F32 = jnp.float32
BF16 = jnp.bfloat16


def _params(sem):
    return pltpu.CompilerParams(dimension_semantics=sem, vmem_limit_bytes=VMEM_LIMIT)


def _const_spec(shape):
    nd = len(shape)
    return pl.BlockSpec(shape, lambda *_: (0,) * nd)


def _layer_norm(v, g, b):
    mu = jnp.mean(v, axis=-1, keepdims=True)
    d = v - mu
    var = jnp.mean(d * d, axis=-1, keepdims=True)
    return d * lax.rsqrt(var + LN_EPS) * g + b


def _shift_rows(v, k):
    n = v.shape[0]
    return pltpu.roll(v, (-k) % n, 0)


def _modulated_with_halo(prev_ref, main_ref, next_ref, shift, scale, nt):
    i = pl.program_id(1)
    keep_prev = (i > 0).astype(F32)
    keep_next = (i < nt - 1).astype(F32)
    hp = (prev_ref[0] * (1.0 + scale) + shift) * keep_prev
    hm = main_ref[0] * (1.0 + scale) + shift
    hn = (next_ref[0] * (1.0 + scale) + shift) * keep_next
    return jnp.concatenate([hp, hm, hn], axis=0)


def _halo_specs(tm, nt, width, order):
    nb = tm // HALO
    last = nt * nb - 1
    main = pl.BlockSpec((1, tm, width), lambda b, i: (b, order(i), 0))
    prev = pl.BlockSpec((1, HALO, width), lambda b, i: (b, jnp.maximum(order(i) * nb - 1, 0), 0))
    nxt = pl.BlockSpec((1, HALO, width), lambda b, i: (b, jnp.minimum((order(i) + 1) * nb, last), 0))
    return prev, main, nxt


def _mod_spec(mod):
    if mod.shape[0] == 1:
        return pl.BlockSpec((1, SUBLANES, D_MODEL), lambda b, i: (0, 0, 0))
    return pl.BlockSpec((1, SUBLANES, D_MODEL), lambda b, i: (b, 0, 0))


def _mod_kernel(c_ref, w_ref, b_ref, o_ref):
    c = c_ref[...]
    sc = c * jax.nn.sigmoid(c)
    o_ref[0] = jnp.dot(sc, w_ref[0], preferred_element_type=F32) + b_ref[0]


def _modulation(c_all, mod_w, mod_b):
    rows = c_all.shape[0]
    tn = 1024
    return pl.pallas_call(
        _mod_kernel,
        grid=(DEPTH, 6 * D_MODEL // tn),
        in_specs=[
            pl.BlockSpec((rows, D_MODEL), lambda l, j: (0, 0)),
            pl.BlockSpec((1, D_MODEL, tn), lambda l, j: (l, 0, j)),
            pl.BlockSpec((1, 1, tn), lambda l, j: (l, 0, j)),
        ],
        out_specs=pl.BlockSpec((1, rows, tn), lambda l, j: (l, 0, j)),
        out_shape=jax.ShapeDtypeStruct((DEPTH, rows, 6 * D_MODEL), F32),
        compiler_params=_params(("arbitrary", "arbitrary")),
        name="modulation",
    )(c_all, mod_w, mod_b.reshape(DEPTH, 1, 6 * D_MODEL))


def _scan_tile(a, b, h0, reverse):
    T = a.shape[0]
    row = lax.broadcasted_iota(jnp.int32, a.shape, 0) & (SUBLANES - 1)
    d = 1
    while d < SUBLANES:
        if reverse:
            a_s, b_s = _shift_rows(a, d), _shift_rows(b, d)
            valid = row < SUBLANES - d
        else:
            a_s, b_s = _shift_rows(a, -d), _shift_rows(b, -d)
            valid = row >= d
        b = jnp.where(valid, a * b_s + b, b)
        a = jnp.where(valid, a * a_s, a)
        d *= 2
    nblk = T // SUBLANES
    out = [None] * nblk
    h = h0
    order = range(nblk - 1, -1, -1) if reverse else range(nblk)
    for j in order:
        sl = slice(j * SUBLANES, (j + 1) * SUBLANES)
        hb = b[sl] + a[sl] * h
        out[j] = hb
        h = hb[0:1] if reverse else hb[SUBLANES - 1:SUBLANES]
    return jnp.concatenate(out, axis=0), h


def _even_fwd_kernel(prev_ref, main_ref, next_ref, mod_ref, w_in_ref, rcw_ref, rcb_ref, wg_ref, bg_ref,
                     lam_ref, scw_ref, scb_ref, h0_ref, mid_ref, hlast_ref, h_sc, *, tm, nt):
    i = pl.program_id(1)

    @pl.when(i == 0)
    def _():
        h_sc[...] = h0_ref[0]

    shift, scale = mod_ref[0, 0:1, :], mod_ref[0, 1:2, :]
    hm = _modulated_with_halo(prev_ref, main_ref, next_ref, shift, scale, nt)
    proj = jnp.dot(hm.astype(BF16), w_in_ref[...], preferred_element_type=F32)
    xr = proj[:, 0:REC_W]
    gr = proj[HALO:HALO + tm, REC_W:2 * REC_W]
    sb = proj[HALO:HALO + tm, 2 * REC_W:2 * REC_W + SC_W]
    sg = proj[:, 2 * REC_W + SC_W:2 * REC_W + 2 * SC_W]
    sx = proj[:, 2 * REC_W + 2 * SC_W:]

    xc = rcb_ref[...] + sum(rcw_ref[k:k + 1, :] * _shift_rows(xr, k - REC_CONV // 2) for k in range(REC_CONV))
    xc = xc[HALO:HALO + tm]

    gates = jnp.dot(xc.astype(BF16), wg_ref[...], preferred_element_type=F32) + bg_ref[...]
    neg_c_sp = -LRU_C * jax.nn.softplus(-lam_ref[...])
    coeffs = []
    for dirn in range(2):
        r = jax.nn.sigmoid(gates[:, (2 * dirn) * REC_W:(2 * dirn + 1) * REC_W])
        g_in = jax.nn.sigmoid(gates[:, (2 * dirn + 1) * REC_W:(2 * dirn + 2) * REC_W])
        log_a = r * neg_c_sp[:, dirn * REC_W:(dirn + 1) * REC_W]
        a = jnp.exp(log_a)
        th = jnp.tanh(log_a)
        mult = jnp.sqrt(-2.0 * th / (1.0 - th))
        coeffs.append((a, mult * (g_in * xc)))

    h_f, h_carry = _scan_tile(coeffs[0][0], coeffs[0][1], h_sc[...], reverse=False)
    h_sc[...] = h_carry
    hlast_ref[0] = h_carry

    s = sg * sx
    sconv = scb_ref[...] + sum(scw_ref[k:k + 1, :] * _shift_rows(s, k - SC_CONV // 2) for k in range(SC_CONV))
    y_sc = sb * sconv[HALO:HALO + tm]

    mid_ref[0, :, 0:REC_W] = h_f
    mid_ref[0, :, REC_W:2 * REC_W] = coeffs[1][0]
    mid_ref[0, :, 2 * REC_W:3 * REC_W] = coeffs[1][1]
    mid_ref[0, :, 3 * REC_W:4 * REC_W] = jax.nn.gelu(gr)
    mid_ref[0, :, 4 * REC_W:5 * REC_W] = y_sc


def _even_fwd(x, mod, w_in, rcw, rcb, wg, bg, lam, scw, scb, h0, tm):
    B, L, _ = x.shape
    nt = L // tm
    prev, main, nxt = _halo_specs(tm, nt, D_MODEL, lambda i: i)
    return pl.pallas_call(
        functools.partial(_even_fwd_kernel, tm=tm, nt=nt),
        grid=(B, nt),
        in_specs=[prev, main, nxt, _mod_spec(mod), _const_spec(w_in.shape), _const_spec(rcw.shape),
                  _const_spec(rcb.shape), _const_spec(wg.shape), _const_spec(bg.shape), _const_spec(lam.shape),
                  _const_spec(scw.shape), _const_spec(scb.shape),
                  pl.BlockSpec((1, 1, REC_W), lambda b, i: (b, 0, 0))],
        out_specs=[pl.BlockSpec((1, tm, EV_MID), lambda b, i: (b, i, 0)),
                   pl.BlockSpec((1, 1, REC_W), lambda b, i: (b, 0, 0))],
        out_shape=[jax.ShapeDtypeStruct((B, L, EV_MID), F32), jax.ShapeDtypeStruct((B, 1, REC_W), F32)],
        scratch_shapes=[pltpu.VMEM((1, REC_W), F32)],
        compiler_params=_params(("arbitrary", "arbitrary")),
        name="even_fwd",
    )(x, x, x, mod, w_in, rcw, rcb, wg, bg, lam, scw, scb, h0)


def _even_bwd_kernel(mid_ref, x_ref, mod_ref, w_out_ref, g_ref, b_ref, h0_ref, o_ref, hlast_ref, h_sc):
    i = pl.program_id(1)

    @pl.when(i == 0)
    def _():
        h_sc[...] = h0_ref[0]

    h_f = mid_ref[0, :, 0:REC_W]
    a_b = mid_ref[0, :, REC_W:2 * REC_W]
    b_b = mid_ref[0, :, 2 * REC_W:3 * REC_W]
    gg = mid_ref[0, :, 3 * REC_W:4 * REC_W]
    y_sc = mid_ref[0, :, 4 * REC_W:5 * REC_W]

    h_b, h_carry = _scan_tile(a_b, b_b, h_sc[...], reverse=True)
    h_sc[...] = h_carry
    hlast_ref[0] = h_carry

    y_rec = (h_f + h_b) * gg
    y = (jnp.dot(y_rec.astype(BF16), w_out_ref[0:REC_W, :], preferred_element_type=F32)
         + jnp.dot(y_sc.astype(BF16), w_out_ref[REC_W:, :], preferred_element_type=F32))
    gate = mod_ref[0, 2:3, :]
    o_ref[0] = _layer_norm(ALPHA * x_ref[0] + gate * y, g_ref[...], b_ref[...])


def _even_bwd(mid, x, mod, w_out, ln_g, ln_b, h0, tm):
    B, L, _ = x.shape
    nt = L // tm
    rev = lambda i: nt - 1 - i
    return pl.pallas_call(
        _even_bwd_kernel,
        grid=(B, nt),
        in_specs=[pl.BlockSpec((1, tm, EV_MID), lambda b, i: (b, rev(i), 0)),
                  pl.BlockSpec((1, tm, D_MODEL), lambda b, i: (b, rev(i), 0)),
                  _mod_spec(mod), _const_spec(w_out.shape), _const_spec(ln_g.shape), _const_spec(ln_b.shape),
                  pl.BlockSpec((1, 1, REC_W), lambda b, i: (b, 0, 0))],
        out_specs=[pl.BlockSpec((1, tm, D_MODEL), lambda b, i: (b, rev(i), 0)),
                   pl.BlockSpec((1, 1, REC_W), lambda b, i: (b, 0, 0))],
        out_shape=[jax.ShapeDtypeStruct((B, L, D_MODEL), F32), jax.ShapeDtypeStruct((B, 1, REC_W), F32)],
        scratch_shapes=[pltpu.VMEM((1, REC_W), F32)],
        compiler_params=_params(("arbitrary", "arbitrary")),
        name="even_bwd",
    )(mid, x, mod, w_out, ln_g, ln_b, h0)


def _ffn_kernel(prev_ref, main_ref, next_ref, mod_ref, wg_ref, wu_ref, cw_ref, cb_ref, wd_ref, g_ref, b_ref,
                o_ref, *, tm, nt):
    shift, scale, gate = mod_ref[0, 3:4, :], mod_ref[0, 4:5, :], mod_ref[0, 5:6, :]
    hm = _modulated_with_halo(prev_ref, main_ref, next_ref, shift, scale, nt).astype(BF16)
    gpre = jnp.dot(hm, wg_ref[...], preferred_element_type=F32)
    conv = cb_ref[...] + sum(cw_ref[k:k + 1, :] * _shift_rows(gpre, k - FFN_CONV // 2) for k in range(FFN_CONV))
    conv = conv[HALO:HALO + tm]
    up = jnp.dot(hm[HALO:HALO + tm], wu_ref[...], preferred_element_type=F32)
    act = (conv * jax.nn.sigmoid(conv)) * up
    f = jnp.dot(act.astype(BF16), wd_ref[...], preferred_element_type=F32)
    o_ref[0] = _layer_norm(ALPHA * main_ref[0] + gate * f, g_ref[...], b_ref[...])


def _ffn(x, mod, w_gate, w_up, conv_w, conv_b, w_down, ln_g, ln_b, tm):
    B, L, _ = x.shape
    nt = L // tm
    prev, main, nxt = _halo_specs(tm, nt, D_MODEL, lambda i: i)
    return pl.pallas_call(
        functools.partial(_ffn_kernel, tm=tm, nt=nt),
        grid=(B, nt),
        in_specs=[prev, main, nxt, _mod_spec(mod), _const_spec(w_gate.shape), _const_spec(w_up.shape),
                  _const_spec(conv_w.shape), _const_spec(conv_b.shape), _const_spec(w_down.shape),
                  _const_spec(ln_g.shape), _const_spec(ln_b.shape)],
        out_specs=pl.BlockSpec((1, tm, D_MODEL), lambda b, i: (b, i, 0)),
        out_shape=jax.ShapeDtypeStruct((B, L, D_MODEL), F32),
        compiler_params=_params(("arbitrary", "arbitrary")),
        name="conv_ffn",
    )(x, x, x, mod, w_gate, w_up, conv_w, conv_b, w_down, ln_g, ln_b)


def _modproj_kernel(x_ref, mod_ref, w_ref, o_ref):
    shift, scale = mod_ref[0, 0:1, :], mod_ref[0, 1:2, :]
    h = x_ref[0] * (1.0 + scale) + shift
    o_ref[0] = jnp.dot(h.astype(BF16), w_ref[...], preferred_element_type=F32).astype(o_ref.dtype)


def _modproj(x, mod, w, tm):
    B, L, _ = x.shape
    n = w.shape[1]
    return pl.pallas_call(
        _modproj_kernel,
        grid=(B, L // tm),
        in_specs=[pl.BlockSpec((1, tm, D_MODEL), lambda b, i: (b, i, 0)), _mod_spec(mod), _const_spec(w.shape)],
        out_specs=pl.BlockSpec((1, tm, n), lambda b, i: (b, i, 0)),
        out_shape=jax.ShapeDtypeStruct((B, L, n), BF16),
        compiler_params=_params(("arbitrary", "arbitrary")),
        name="mod_proj",
    )(x, mod, w)


HEAD_GROUP = 4
GROUP_W = HEAD_GROUP * NA_HEAD_DIM
N_GROUPS = NA_HEADS // HEAD_GROUP
N_DR = 2 * NA_WIN_R - 1
N_DC = 2 * NA_WIN_C - 1


def _attn_kernel(q_ref, k_ref, v_ref, kc_ref, vc_ref, bias_ref, o_ref, *, rows):
    win_keys = NA_WIN_R * GRID_W
    head_of_lane = lax.broadcasted_iota(jnp.int32, (GRID_W, GROUP_W), 1) // NA_HEAD_DIM
    scale = NA_HEAD_DIM ** -0.5
    kc = kc_ref[0]
    vc = vc_ref[0]
    dims = (((1,), (1,)), ((), ()))

    def row_body(r, carry):
        rs = jnp.clip(r - NA_WIN_R // 2, 0, rows - NA_WIN_R)
        cls = rs - r + NA_WIN_R - 1
        q = q_ref[0, pl.ds(pl.multiple_of(r * GRID_W, GRID_W), GRID_W), :] * scale
        lhs = jnp.concatenate([jnp.where(head_of_lane == hh, q, jnp.zeros_like(q)) for hh in range(HEAD_GROUP)],
                              axis=0)
        k_loc = k_ref[0, pl.ds(pl.multiple_of(rs * GRID_W, GRID_W), win_keys), :]
        v_loc = v_ref[0, pl.ds(pl.multiple_of(rs * GRID_W, GRID_W), win_keys), :]
        s_loc = lax.dot_general(lhs, k_loc, dims, preferred_element_type=F32) + bias_ref[0, cls]
        s_ctx = lax.dot_general(lhs, kc, dims, preferred_element_type=F32)
        m = jnp.maximum(jnp.max(s_loc, axis=-1, keepdims=True), jnp.max(s_ctx, axis=-1, keepdims=True))
        p_loc = jnp.exp(s_loc - m)
        p_ctx = jnp.exp(s_ctx - m)
        denom = jnp.sum(p_loc, axis=-1, keepdims=True) + jnp.sum(p_ctx, axis=-1, keepdims=True)
        o_all = (jnp.dot(p_loc.astype(BF16), v_loc, preferred_element_type=F32)
                 + jnp.dot(p_ctx.astype(BF16), vc, preferred_element_type=F32)) * (1.0 / denom)
        o = o_all[0:GRID_W]
        for hh in range(1, HEAD_GROUP):
            o = jnp.where(head_of_lane == hh, o_all[hh * GRID_W:(hh + 1) * GRID_W], o)
        o_ref[0, pl.ds(pl.multiple_of(r * GRID_W, GRID_W), GRID_W), :] = o.astype(o_ref.dtype)
        return carry

    lax.fori_loop(0, rows, row_body, 0, unroll=4)


def _attention(qkv, kvc, bias):
    B, S, _ = qkv.shape
    Lc = kvc.shape[1]
    rows = S // GRID_W
    return pl.pallas_call(
        functools.partial(_attn_kernel, rows=rows),
        grid=(N_GROUPS, B),
        in_specs=[pl.BlockSpec((1, S, GROUP_W), lambda g, b: (b, 0, g)),
                  pl.BlockSpec((1, S, GROUP_W), lambda g, b: (b, 0, N_GROUPS + g)),
                  pl.BlockSpec((1, S, GROUP_W), lambda g, b: (b, 0, 2 * N_GROUPS + g)),
                  pl.BlockSpec((1, Lc, GROUP_W), lambda g, b: (b, 0, g)),
                  pl.BlockSpec((1, Lc, GROUP_W), lambda g, b: (b, 0, N_GROUPS + g)),
                  pl.BlockSpec((1, NA_WIN_R, HEAD_GROUP * GRID_W, NA_WIN_R * GRID_W), lambda g, b: (g, 0, 0, 0))],
        out_specs=pl.BlockSpec((1, S, GROUP_W), lambda g, b: (b, 0, g)),
        out_shape=jax.ShapeDtypeStruct((B, S, NA_W), BF16),
        compiler_params=_params(("arbitrary", "arbitrary")),
        name="nbr_attention",
    )(qkv, qkv, qkv, kvc, kvc, bias)


def _bias_kernel(rpb_ref, o_ref):
    h = pl.program_id(0)
    q = lax.broadcasted_iota(jnp.int32, (GRID_W, GRID_W), 0)
    k = lax.broadcasted_iota(jnp.int32, (GRID_W, GRID_W), 1)
    dc = jnp.clip(k - q, -(NA_WIN_C - 1), NA_WIN_C - 1) + NA_WIN_C - 1
    col_start = jnp.clip(q - NA_WIN_C // 2, 0, GRID_W - NA_WIN_C)
    in_win = (k >= col_start) & (k < col_start + NA_WIN_C)
    for dr in range(N_DR):
        acc = jnp.zeros((GRID_W, GRID_W), F32)
        for c in range(N_DC):
            acc = jnp.where(dc == c, rpb_ref[(h * N_DR + dr) * N_DC + c], acc)
        o_ref[0, dr] = jnp.where(in_win, acc, NEG_INF)


def _attn_bias_table(rpb):
    t = pl.pallas_call(
        _bias_kernel,
        grid=(NA_HEADS,),
        in_specs=[pl.BlockSpec(memory_space=pltpu.SMEM)],
        out_specs=pl.BlockSpec((1, N_DR, GRID_W, GRID_W), lambda h: (h, 0, 0, 0)),
        out_shape=jax.ShapeDtypeStruct((NA_HEADS, N_DR, GRID_W, GRID_W), F32),
        compiler_params=_params(("arbitrary",)),
        name="attn_bias",
    )(rpb.astype(F32).reshape(-1))
    t = jnp.stack([t[:, c:c + NA_WIN_R] for c in range(NA_WIN_R)], axis=1)
    t = t.reshape(N_GROUPS, HEAD_GROUP, NA_WIN_R, NA_WIN_R, GRID_W, GRID_W)
    t = t.transpose(0, 2, 1, 4, 3, 5)
    return t.reshape(N_GROUPS, NA_WIN_R, HEAD_GROUP * GRID_W, NA_WIN_R * GRID_W)


def _outproj_kernel(a_ref, x_ref, mod_ref, w_ref, g_ref, b_ref, o_ref):
    y = jnp.dot(a_ref[0], w_ref[...], preferred_element_type=F32)
    gate = mod_ref[0, 2:3, :]
    o_ref[0] = _layer_norm(ALPHA * x_ref[0] + gate * y, g_ref[...], b_ref[...])


def _outproj(a, x, mod, w, ln_g, ln_b, tm):
    B, L, _ = x.shape
    return pl.pallas_call(
        _outproj_kernel,
        grid=(B, L // tm),
        in_specs=[pl.BlockSpec((1, tm, NA_W), lambda b, i: (b, i, 0)),
                  pl.BlockSpec((1, tm, D_MODEL), lambda b, i: (b, i, 0)),
                  _mod_spec(mod), _const_spec(w.shape), _const_spec(ln_g.shape), _const_spec(ln_b.shape)],
        out_specs=pl.BlockSpec((1, tm, D_MODEL), lambda b, i: (b, i, 0)),
        out_shape=jax.ShapeDtypeStruct((B, L, D_MODEL), F32),
        compiler_params=_params(("arbitrary", "arbitrary")),
        name="attn_out_proj",
    )(a, x, mod, w, ln_g, ln_b)


def _block_diag(w):
    eye = jnp.eye(REC_BLOCKS, dtype=w.dtype)
    return (w[:, :, None, :] * eye[:, None, :, None]).reshape(REC_W, REC_W)


def _pad_rows(w, rows=SUBLANES):
    return jnp.pad(w, ((0, rows - w.shape[0]), (0, 0)))


def _mod_rows(m):
    m = m.reshape(m.shape[:-1] + (6, D_MODEL))
    return jnp.pad(m, [(0, 0)] * (m.ndim - 2) + [(0, 2), (0, 0)])


def kernel(x, c, ctx, c_ctx, mod_w, mod_b, ln1_g, ln1_b, ln2_g, ln2_b, ev_w_in, ev_w_out, rec_conv_w, rec_conv_b,
           rec_wa, rec_ba, rec_wx, rec_bx, rec_lam, sc_conv_w, sc_conv_b, na_w_qkv, na_w_out, na_rpb, ffn_w_gate,
           ffn_w_up, ffn_conv_w, ffn_conv_b, ffn_w_down):
    B = x.shape[0]
    tm_x, tm_c = 256, 256

    c_all = jnp.concatenate([c, c_ctx[None, :], jnp.zeros((SUBLANES - 1, D_MODEL), F32)], axis=0)
    mods = _modulation(c_all, mod_w, mod_b)
    mod_x = [_mod_rows(mods[i, :B]) for i in range(DEPTH)]
    mod_c = [_mod_rows(mods[i, B:B + 1]) for i in range(DEPTH)]

    row = lambda v: v.reshape(1, -1).astype(F32)
    ffn = lambda i: (ffn_w_gate[i].astype(BF16), ffn_w_up[i].astype(BF16), _pad_rows(ffn_conv_w[i]),
                     row(ffn_conv_b[i]), ffn_w_down[i].astype(BF16), row(ln2_g[i]), row(ln2_b[i]))

    w_in = ev_w_in[0].astype(BF16)
    w_out = ev_w_out[0].astype(BF16)
    wg = jnp.concatenate([_block_diag(rec_wa[0, 0]), _block_diag(rec_wx[0, 0]),
                          _block_diag(rec_wa[0, 1]), _block_diag(rec_wx[0, 1])], axis=1).astype(BF16)
    bg = jnp.concatenate([rec_ba[0, 0], rec_bx[0, 0], rec_ba[0, 1], rec_bx[0, 1]]).reshape(1, -1)
    lam = rec_lam[0].reshape(1, 2 * REC_W)
    rec_args = (w_in, _pad_rows(rec_conv_w[0]), row(rec_conv_b[0]), wg, bg, lam, _pad_rows(sc_conv_w[0]),
                row(sc_conv_b[0]))
    zeros = jnp.zeros((B, 1, REC_W), F32)

    mid_c, last_f = _even_fwd(ctx, mod_c[0], *rec_args, zeros, tm_c)
    mid_x, _ = _even_fwd(x, mod_x[0], *rec_args, last_f, tm_x)
    ctx, last_b = _even_bwd(mid_c, ctx, mod_c[0], w_out, row(ln1_g[0]), row(ln1_b[0]), zeros, tm_c)
    x, _ = _even_bwd(mid_x, x, mod_x[0], w_out, row(ln1_g[0]), row(ln1_b[0]), last_b, tm_x)
    ctx = _ffn(ctx, mod_c[0], *ffn(0), tm_c)
    x = _ffn(x, mod_x[0], *ffn(0), tm_x)

    w_qkv = na_w_qkv[0].astype(BF16)
    qkv = _modproj(x, mod_x[1], w_qkv, tm_x)
    kvc = _modproj(ctx, mod_c[1], w_qkv[:, NA_W:], tm_c)
    att = _attention(qkv, kvc, _attn_bias_table(na_rpb[0]))
    x = _outproj(att, x, mod_x[1], na_w_out[0].astype(BF16), row(ln1_g[1]), row(ln1_b[1]), tm_x)
    x = _ffn(x, mod_x[1], *ffn(1), tm_x)
    return x
```

```python
import functools

import jax
import jax.numpy as jnp
from jax import lax
from jax.experimental import pallas as pl
from jax.experimental.pallas import tpu as pltpu

D_MODEL = 1024
DEPTH = 2
GRID_W = 64
REC_W = 512
REC_BLOCKS = 8
REC_BLK = REC_W // REC_BLOCKS
REC_CONV = 4
LRU_C = 8.0
SC_W = 512
SC_CONV = 3
NA_HEADS = 16
NA_HEAD_DIM = 64
NA_W = NA_HEADS * NA_HEAD_DIM
NA_WIN_R = 8
NA_WIN_C = 16
D_FF = 2816
FFN_CONV = 3
EV_IN = 2 * REC_W + 3 * SC_W
EV_MID = 5 * REC_W
ALPHA = (2.0 * DEPTH) ** 0.25
LN_EPS = 1e-6
NEG_INF = -1e30

SUBLANES = 8
LANES = 128
HALO = SUBLANES
VMEM_LIMIT = 56 * 1024 * 1024

F32 = jnp.float32
BF16 = jnp.bfloat16


def _params(sem):
    return pltpu.CompilerParams(dimension_semantics=sem, vmem_limit_bytes=VMEM_LIMIT)


def _const_spec(shape):
    nd = len(shape)
    return pl.BlockSpec(shape, lambda *_: (0,) * nd, pipeline_mode=pl.Buffered(1))


def _layer_norm(v, g, b):
    mu = jnp.mean(v, axis=-1, keepdims=True)
    d = v - mu
    var = jnp.mean(d * d, axis=-1, keepdims=True)
    return d * lax.rsqrt(var + LN_EPS) * g + b


def _shift_rows(v, k):
    n = v.shape[0]
    return pltpu.roll(v, (-k) % n, 0)


def _dwconv_rows(v, w_ref, b_ref, taps, before, tm):
    return b_ref[...] + sum(w_ref[k:k + 1, :] * _shift_rows(v, k - before)[HALO:HALO + tm] for k in range(taps))


def _with_halo(prev_ref, main_ref, next_ref):
    prev = prev_ref[0].astype(F32)
    nxt = next_ref[0].astype(F32)
    return jnp.concatenate([prev[prev.shape[0] - HALO:], main_ref[0].astype(F32), nxt[:HALO]], axis=0)


def _modulated_with_halo(x_rows, shift, scale, tm, nt):
    i = pl.program_id(1)
    keep_prev = (i > 0).astype(F32)
    keep_next = (i < nt - 1).astype(F32)
    h = x_rows * (1.0 + scale) + shift
    return jnp.concatenate([h[:HALO] * keep_prev, h[HALO:HALO + tm], h[HALO + tm:] * keep_next], axis=0)


def _halo_specs(tm, nt, width, halo_rows=HALO):
    nb = tm // halo_rows
    last = nt * nb - 1
    main = pl.BlockSpec((1, tm, width), lambda b, i: (b, i, 0))
    prev = pl.BlockSpec((1, halo_rows, width), lambda b, i: (b, jnp.maximum(i * nb - 1, 0), 0))
    nxt = pl.BlockSpec((1, halo_rows, width), lambda b, i: (b, jnp.minimum((i + 1) * nb, last), 0))
    return prev, main, nxt


def _mod_spec(mod):
    if mod.shape[0] == 1:
        return pl.BlockSpec((1, SUBLANES, D_MODEL), lambda b, i: (0, 0, 0))
    return pl.BlockSpec((1, SUBLANES, D_MODEL), lambda b, i: (b, 0, 0))


def _mod_kernel(c_ref, w_ref, b_ref, o_ref):
    c = c_ref[...]
    sc = c * jax.nn.sigmoid(c)
    o_ref[0] = jnp.dot(sc, w_ref[0], preferred_element_type=F32) + b_ref[0]


def _modulation(c_all, mod_w, mod_b):
    rows = c_all.shape[0]
    tn = 1024
    return pl.pallas_call(
        _mod_kernel,
        grid=(DEPTH, 6 * D_MODEL // tn),
        in_specs=[
            pl.BlockSpec((rows, D_MODEL), lambda l, j: (0, 0)),
            pl.BlockSpec((1, D_MODEL, tn), lambda l, j: (l, 0, j)),
            pl.BlockSpec((1, 1, tn), lambda l, j: (l, 0, j)),
        ],
        out_specs=pl.BlockSpec((1, rows, tn), lambda l, j: (l, 0, j)),
        out_shape=jax.ShapeDtypeStruct((DEPTH, rows, 6 * D_MODEL), F32),
        compiler_params=_params(("arbitrary", "arbitrary")),
        name="modulation",
    )(c_all, mod_w, mod_b.reshape(DEPTH, 1, 6 * D_MODEL))


def _scan_tile(a, b, h0, reverse):
    T = a.shape[0]
    row = lax.broadcasted_iota(jnp.int32, a.shape, 0) & (SUBLANES - 1)
    d = 1
    while d < SUBLANES:
        if reverse:
            a_s, b_s = _shift_rows(a, d), _shift_rows(b, d)
            valid = row < SUBLANES - d
        else:
            a_s, b_s = _shift_rows(a, -d), _shift_rows(b, -d)
            valid = row >= d
        b = jnp.where(valid, a * b_s + b, b)
        a = jnp.where(valid, a * a_s, a)
        d *= 2
    nblk = T // SUBLANES
    out = [None] * nblk
    h = h0
    order = range(nblk - 1, -1, -1) if reverse else range(nblk)
    for j in order:
        sl = slice(j * SUBLANES, (j + 1) * SUBLANES)
        hb = b[sl] + a[sl] * h
        out[j] = hb
        h = hb[0:1] if reverse else hb[SUBLANES - 1:SUBLANES]
    return jnp.concatenate(out, axis=0), h


def _even_fwd_kernel(prev_ref, main_ref, next_ref, mod_ref, w_in_ref, rcw_ref, rcb_ref, wg_ref, bg_ref,
                     lam_ref, scw_ref, scb_ref, h0_ref, mid_ref, hlast_ref, h_sc, *, tm, nt):
    i = pl.program_id(1)

    @pl.when(i == 0)
    def _():
        h_sc[...] = h0_ref[0]

    shift, scale = mod_ref[0, 0:1, :], mod_ref[0, 1:2, :]
    hm = _modulated_with_halo(_with_halo(prev_ref, main_ref, next_ref), shift, scale, tm, nt)
    proj = jnp.dot(hm.astype(BF16), w_in_ref[...], preferred_element_type=F32)
    xr = proj[:, 0:REC_W]
    gr = proj[HALO:HALO + tm, REC_W:2 * REC_W]
    sb = proj[HALO:HALO + tm, 2 * REC_W:2 * REC_W + SC_W]
    sg = proj[:, 2 * REC_W + SC_W:2 * REC_W + 2 * SC_W]
    sx = proj[:, 2 * REC_W + 2 * SC_W:]

    xc = _dwconv_rows(xr, rcw_ref, rcb_ref, REC_CONV, REC_CONV // 2, tm)

    gates = jnp.dot(xc.astype(BF16), wg_ref[...], preferred_element_type=F32) + bg_ref[...]
    neg_c_sp = -LRU_C * jax.nn.softplus(-lam_ref[...])
    coeffs = []
    for dirn in range(2):
        r = jax.nn.sigmoid(gates[:, (2 * dirn) * REC_W:(2 * dirn + 1) * REC_W])
        g_in = jax.nn.sigmoid(gates[:, (2 * dirn + 1) * REC_W:(2 * dirn + 2) * REC_W])
        log_a = r * neg_c_sp[:, dirn * REC_W:(dirn + 1) * REC_W]
        a = jnp.exp(log_a)
        th = jnp.tanh(log_a)
        mult = jnp.sqrt(-2.0 * th / (1.0 - th))
        coeffs.append((a, mult * (g_in * xc)))

    h_f, h_carry = _scan_tile(coeffs[0][0], coeffs[0][1], h_sc[...], reverse=False)
    h_sc[...] = h_carry
    hlast_ref[0] = h_carry

    y_sc = sb * _dwconv_rows(sg * sx, scw_ref, scb_ref, SC_CONV, SC_CONV // 2, tm)

    mid_ref[0, :, 0:REC_W] = h_f
    mid_ref[0, :, REC_W:2 * REC_W] = coeffs[1][0]
    mid_ref[0, :, 2 * REC_W:3 * REC_W] = coeffs[1][1]
    mid_ref[0, :, 3 * REC_W:4 * REC_W] = jax.nn.gelu(gr)
    mid_ref[0, :, 4 * REC_W:5 * REC_W] = y_sc


def _even_fwd(x, mod, w_in, rcw, rcb, wg, bg, lam, scw, scb, h0, tm):
    B, L, _ = x.shape
    nt = L // tm
    prev, main, nxt = _halo_specs(tm, nt, D_MODEL)
    return pl.pallas_call(
        functools.partial(_even_fwd_kernel, tm=tm, nt=nt),
        grid=(B, nt),
        in_specs=[prev, main, nxt, _mod_spec(mod), _const_spec(w_in.shape), _const_spec(rcw.shape),
                  _const_spec(rcb.shape), _const_spec(wg.shape), _const_spec(bg.shape), _const_spec(lam.shape),
                  _const_spec(scw.shape), _const_spec(scb.shape),
                  pl.BlockSpec((1, 1, REC_W), lambda b, i: (b, 0, 0))],
        out_specs=[pl.BlockSpec((1, tm, EV_MID), lambda b, i: (b, i, 0)),
                   pl.BlockSpec((1, 1, REC_W), lambda b, i: (b, 0, 0))],
        out_shape=[jax.ShapeDtypeStruct((B, L, EV_MID), F32), jax.ShapeDtypeStruct((B, 1, REC_W), F32)],
        scratch_shapes=[pltpu.VMEM((1, REC_W), F32)],
        compiler_params=_params(("arbitrary", "arbitrary")),
        name="even_fwd",
    )(x, x, x, mod, w_in, rcw, rcb, wg, bg, lam, scw, scb, h0)


def _even_bwd_kernel(mid_ref, x_ref, mod_ref, w_out_ref, g_ref, b_ref, h0_ref, o_ref, hlast_ref, h_sc):
    i = pl.program_id(1)

    @pl.when(i == 0)
    def _():
        h_sc[...] = h0_ref[0]

    h_f = mid_ref[0, :, 0:REC_W]
    a_b = mid_ref[0, :, REC_W:2 * REC_W]
    b_b = mid_ref[0, :, 2 * REC_W:3 * REC_W]
    gg = mid_ref[0, :, 3 * REC_W:4 * REC_W]
    y_sc = mid_ref[0, :, 4 * REC_W:5 * REC_W]

    h_b, h_carry = _scan_tile(a_b, b_b, h_sc[...], reverse=True)
    h_sc[...] = h_carry
    hlast_ref[0] = h_carry

    y_rec = (h_f + h_b) * gg
    y = (jnp.dot(y_rec.astype(BF16), w_out_ref[0:REC_W, :], preferred_element_type=F32)
         + jnp.dot(y_sc.astype(BF16), w_out_ref[REC_W:, :], preferred_element_type=F32))
    gate = mod_ref[0, 2:3, :]
    o_ref[0] = _layer_norm(ALPHA * x_ref[0] + gate * y, g_ref[...], b_ref[...])


def _even_bwd(mid, x, mod, w_out, ln_g, ln_b, h0, tm):
    B, L, _ = x.shape
    nt = L // tm
    rev = lambda i: nt - 1 - i
    return pl.pallas_call(
        _even_bwd_kernel,
        grid=(B, nt),
        in_specs=[pl.BlockSpec((1, tm, EV_MID), lambda b, i: (b, rev(i), 0)),
                  pl.BlockSpec((1, tm, D_MODEL), lambda b, i: (b, rev(i), 0)),
                  _mod_spec(mod), _const_spec(w_out.shape), _const_spec(ln_g.shape), _const_spec(ln_b.shape),
                  pl.BlockSpec((1, 1, REC_W), lambda b, i: (b, 0, 0))],
        out_specs=[pl.BlockSpec((1, tm, D_MODEL), lambda b, i: (b, rev(i), 0)),
                   pl.BlockSpec((1, 1, REC_W), lambda b, i: (b, 0, 0))],
        out_shape=[jax.ShapeDtypeStruct((B, L, D_MODEL), F32), jax.ShapeDtypeStruct((B, 1, REC_W), F32)],
        scratch_shapes=[pltpu.VMEM((1, REC_W), F32)],
        compiler_params=_params(("arbitrary", "arbitrary")),
        name="even_bwd",
    )(mid, x, mod, w_out, ln_g, ln_b, h0)


def _ffn_kernel(*refs, tm, nt, mixer_in):
    if mixer_in:
        (aprev_ref, amain_ref, anext_ref, wo_ref, g1_ref, b1_ref), refs = refs[:6], refs[6:]
    prev_ref, main_ref, next_ref, mod_ref, wg_ref, wu_ref, cw_ref, cb_ref, wd_ref, g_ref, b_ref, o_ref = refs
    x_rows = _with_halo(prev_ref, main_ref, next_ref)
    if mixer_in:
        heads = _with_halo(aprev_ref, amain_ref, anext_ref).astype(BF16)
        y = jnp.dot(heads, wo_ref[...], preferred_element_type=F32)
        x_rows = _layer_norm(ALPHA * x_rows + mod_ref[0, 2:3, :] * y, g1_ref[...], b1_ref[...])
    shift, scale, gate = mod_ref[0, 3:4, :], mod_ref[0, 4:5, :], mod_ref[0, 5:6, :]
    hm = _modulated_with_halo(x_rows, shift, scale, tm, nt).astype(BF16)
    gpre = jnp.dot(hm, wg_ref[...], preferred_element_type=F32)
    conv = _dwconv_rows(gpre, cw_ref, cb_ref, FFN_CONV, FFN_CONV // 2, tm)
    up = jnp.dot(hm[HALO:HALO + tm], wu_ref[...], preferred_element_type=F32)
    act = (conv * jax.nn.sigmoid(conv)) * up
    f = jnp.dot(act.astype(BF16), wd_ref[...], preferred_element_type=F32)
    o_ref[0] = _layer_norm(ALPHA * x_rows[HALO:HALO + tm] + gate * f, g_ref[...], b_ref[...])


def _ffn(x, mod, w_gate, w_up, conv_w, conv_b, w_down, ln_g, ln_b, tm, mixer=None):
    B, L, _ = x.shape
    nt = L // tm
    consts = (w_gate, w_up, conv_w, conv_b, w_down, ln_g, ln_b)
    args = (x, x, x, mod) + consts
    specs = list(_halo_specs(tm, nt, D_MODEL)) + [_mod_spec(mod)] + [_const_spec(a.shape) for a in consts]
    if mixer is not None:
        heads, w_out, ln1_g, ln1_b = mixer
        args = (heads, heads, heads, w_out, ln1_g, ln1_b) + args
        specs = (list(_halo_specs(tm, nt, NA_W, halo_rows=2 * HALO))
                 + [_const_spec(a.shape) for a in (w_out, ln1_g, ln1_b)] + specs)
    return pl.pallas_call(
        functools.partial(_ffn_kernel, tm=tm, nt=nt, mixer_in=mixer is not None),
        grid=(B, nt),
        in_specs=specs,
        out_specs=pl.BlockSpec((1, tm, D_MODEL), lambda b, i: (b, i, 0)),
        out_shape=jax.ShapeDtypeStruct((B, L, D_MODEL), F32),
        compiler_params=_params(("arbitrary", "arbitrary")),
        name="conv_ffn",
    )(*args)


def _modproj_kernel(x_ref, mod_ref, w_ref, o_ref):
    shift, scale = mod_ref[0, 0:1, :], mod_ref[0, 1:2, :]
    h = x_ref[0] * (1.0 + scale) + shift
    o_ref[0] = jnp.dot(h.astype(BF16), w_ref[...], preferred_element_type=F32).astype(o_ref.dtype)


def _modproj(x, mod, w, tm):
    B, L, _ = x.shape
    n = w.shape[1]
    return pl.pallas_call(
        _modproj_kernel,
        grid=(B, L // tm),
        in_specs=[pl.BlockSpec((1, tm, D_MODEL), lambda b, i: (b, i, 0)), _mod_spec(mod), _const_spec(w.shape)],
        out_specs=pl.BlockSpec((1, tm, n), lambda b, i: (b, i, 0)),
        out_shape=jax.ShapeDtypeStruct((B, L, n), BF16),
        compiler_params=_params(("arbitrary", "arbitrary")),
        name="mod_proj",
    )(x, mod, w)


HEAD_GROUP = 4
GROUP_W = HEAD_GROUP * NA_HEAD_DIM
N_GROUPS = NA_HEADS // HEAD_GROUP
N_DR = 2 * NA_WIN_R - 1
N_DC = 2 * NA_WIN_C - 1


def _attn_kernel(q_ref, k_ref, v_ref, kc_ref, vc_ref, bias_ref, o_ref, *, rows):
    win_keys = NA_WIN_R * GRID_W
    head_of_lane = lax.broadcasted_iota(jnp.int32, (GRID_W, GROUP_W), 1) // NA_HEAD_DIM
    scale = NA_HEAD_DIM ** -0.5
    kc = kc_ref[0]
    vc = vc_ref[0]
    dims = (((1,), (1,)), ((), ()))

    def row_body(r, carry):
        rs = jnp.clip(r - NA_WIN_R // 2, 0, rows - NA_WIN_R)
        cls = rs - r + NA_WIN_R - 1
        q = q_ref[0, pl.ds(pl.multiple_of(r * GRID_W, GRID_W), GRID_W), :] * scale
        lhs = jnp.concatenate([jnp.where(head_of_lane == hh, q, jnp.zeros_like(q)) for hh in range(HEAD_GROUP)],
                              axis=0)
        k_loc = k_ref[0, pl.ds(pl.multiple_of(rs * GRID_W, GRID_W), win_keys), :]
        v_loc = v_ref[0, pl.ds(pl.multiple_of(rs * GRID_W, GRID_W), win_keys), :]
        s_loc = lax.dot_general(lhs, k_loc, dims, preferred_element_type=F32) + bias_ref[0, cls]
        s_ctx = lax.dot_general(lhs, kc, dims, preferred_element_type=F32)
        s = jnp.concatenate([s_loc, s_ctx], axis=-1)
        p = jnp.exp(s - jnp.max(s, axis=-1, keepdims=True))
        denom = jnp.sum(p, axis=-1, keepdims=True)
        p = p.astype(BF16)
        o_all = (jnp.dot(p[:, :win_keys], v_loc, preferred_element_type=F32)
                 + jnp.dot(p[:, win_keys:], vc, preferred_element_type=F32)) * (1.0 / denom)
        o = o_all[0:GRID_W]
        for hh in range(1, HEAD_GROUP):
            o = jnp.where(head_of_lane == hh, o_all[hh * GRID_W:(hh + 1) * GRID_W], o)
        o_ref[0, pl.ds(pl.multiple_of(r * GRID_W, GRID_W), GRID_W), :] = o.astype(o_ref.dtype)
        return carry

    lax.fori_loop(0, rows, row_body, 0, unroll=16)


def _attention(qkv, kvc, bias):
    B, S, _ = qkv.shape
    Lc = kvc.shape[1]
    rows = S // GRID_W
    return pl.pallas_call(
        functools.partial(_attn_kernel, rows=rows),
        grid=(N_GROUPS, B),
        in_specs=[pl.BlockSpec((1, S, GROUP_W), lambda g, b: (b, 0, g)),
                  pl.BlockSpec((1, S, GROUP_W), lambda g, b: (b, 0, N_GROUPS + g)),
                  pl.BlockSpec((1, S, GROUP_W), lambda g, b: (b, 0, 2 * N_GROUPS + g)),
                  pl.BlockSpec((1, Lc, GROUP_W), lambda g, b: (b, 0, g)),
                  pl.BlockSpec((1, Lc, GROUP_W), lambda g, b: (b, 0, N_GROUPS + g)),
                  pl.BlockSpec((1, NA_WIN_R, HEAD_GROUP * GRID_W, NA_WIN_R * GRID_W), lambda g, b: (g, 0, 0, 0))],
        out_specs=pl.BlockSpec((1, S, GROUP_W), lambda g, b: (b, 0, g)),
        out_shape=jax.ShapeDtypeStruct((B, S, NA_W), BF16),
        compiler_params=_params(("arbitrary", "arbitrary")),
        name="nbr_attention",
    )(qkv, qkv, qkv, kvc, kvc, bias)


def _bias_kernel(rpb_ref, o_ref):
    h = pl.program_id(0)
    q = lax.broadcasted_iota(jnp.int32, (GRID_W, GRID_W), 0)
    k = lax.broadcasted_iota(jnp.int32, (GRID_W, GRID_W), 1)
    dc = jnp.clip(k - q, -(NA_WIN_C - 1), NA_WIN_C - 1) + NA_WIN_C - 1
    col_start = jnp.clip(q - NA_WIN_C // 2, 0, GRID_W - NA_WIN_C)
    in_win = (k >= col_start) & (k < col_start + NA_WIN_C)
    for dr in range(N_DR):
        acc = jnp.zeros((GRID_W, GRID_W), F32)
        for c in range(N_DC):
            acc = jnp.where(dc == c, rpb_ref[(h * N_DR + dr) * N_DC + c], acc)
        o_ref[0, dr] = jnp.where(in_win, acc, NEG_INF)


def _attn_bias_table(rpb):
    t = pl.pallas_call(
        _bias_kernel,
        grid=(NA_HEADS,),
        in_specs=[pl.BlockSpec(memory_space=pltpu.SMEM)],
        out_specs=pl.BlockSpec((1, N_DR, GRID_W, GRID_W), lambda h: (h, 0, 0, 0)),
        out_shape=jax.ShapeDtypeStruct((NA_HEADS, N_DR, GRID_W, GRID_W), F32),
        compiler_params=_params(("arbitrary",)),
        name="attn_bias",
    )(rpb.astype(F32).reshape(-1))
    t = jnp.stack([t[:, c:c + NA_WIN_R] for c in range(NA_WIN_R)], axis=1)
    t = t.reshape(N_GROUPS, HEAD_GROUP, NA_WIN_R, NA_WIN_R, GRID_W, GRID_W)
    t = t.transpose(0, 2, 1, 4, 3, 5)
    return t.reshape(N_GROUPS, NA_WIN_R, HEAD_GROUP * GRID_W, NA_WIN_R * GRID_W)


def _block_diag(w):
    eye = jnp.eye(REC_BLOCKS, dtype=w.dtype)
    return (w[:, :, None, :] * eye[:, None, :, None]).reshape(REC_W, REC_W)


def _pad_rows(w, rows=SUBLANES):
    return jnp.pad(w, ((0, rows - w.shape[0]), (0, 0)))


def _mod_rows(m):
    m = m.reshape(m.shape[:-1] + (6, D_MODEL))
    return jnp.pad(m, [(0, 0)] * (m.ndim - 2) + [(0, 2), (0, 0)])


def kernel(x, c, ctx, c_ctx, mod_w, mod_b, ln1_g, ln1_b, ln2_g, ln2_b, ev_w_in, ev_w_out, rec_conv_w, rec_conv_b,
           rec_wa, rec_ba, rec_wx, rec_bx, rec_lam, sc_conv_w, sc_conv_b, na_w_qkv, na_w_out, na_rpb, ffn_w_gate,
           ffn_w_up, ffn_conv_w, ffn_conv_b, ffn_w_down):
    B = x.shape[0]
    tm_x, tm_c = 256, 256
    tm_mm = 512

    c_all = jnp.concatenate([c, c_ctx[None, :], jnp.zeros((SUBLANES - 1, D_MODEL), F32)], axis=0)
    mods = _modulation(c_all, mod_w, mod_b)
    mod_x = [_mod_rows(mods[i, :B]) for i in range(DEPTH)]
    mod_c = [_mod_rows(mods[i, B:B + 1]) for i in range(DEPTH)]

    row = lambda v: v.reshape(1, -1).astype(F32)
    ffn = lambda i: (ffn_w_gate[i].astype(BF16), ffn_w_up[i].astype(BF16), _pad_rows(ffn_conv_w[i]),
                     row(ffn_conv_b[i]), ffn_w_down[i].astype(BF16), row(ln2_g[i]), row(ln2_b[i]))

    w_in = ev_w_in[0].astype(BF16)
    w_out = ev_w_out[0].astype(BF16)
    wg = jnp.concatenate([_block_diag(rec_wa[0, 0]), _block_diag(rec_wx[0, 0]),
                          _block_diag(rec_wa[0, 1]), _block_diag(rec_wx[0, 1])], axis=1).astype(BF16)
    bg = jnp.concatenate([rec_ba[0, 0], rec_bx[0, 0], rec_ba[0, 1], rec_bx[0, 1]]).reshape(1, -1)
    lam = rec_lam[0].reshape(1, 2 * REC_W)
    rec_args = (w_in, _pad_rows(rec_conv_w[0]), row(rec_conv_b[0]), wg, bg, lam, _pad_rows(sc_conv_w[0]),
                row(sc_conv_b[0]))
    zeros = jnp.zeros((B, 1, REC_W), F32)

    mid_c, last_f = _even_fwd(ctx, mod_c[0], *rec_args, zeros, tm_c)
    mid_x, _ = _even_fwd(x, mod_x[0], *rec_args, last_f, tm_x)
    ctx, last_b = _even_bwd(mid_c, ctx, mod_c[0], w_out, row(ln1_g[0]), row(ln1_b[0]), zeros, tm_c)
    x, _ = _even_bwd(mid_x, x, mod_x[0], w_out, row(ln1_g[0]), row(ln1_b[0]), last_b, tm_x)
    ctx = _ffn(ctx, mod_c[0], *ffn(0), tm_c)
    x = _ffn(x, mod_x[0], *ffn(0), tm_mm)

    w_qkv = na_w_qkv[0].astype(BF16)
    qkv = _modproj(x, mod_x[1], w_qkv, tm_mm)
    kvc = _modproj(ctx, mod_c[1], w_qkv[:, NA_W:], tm_c)
    att = _attention(qkv, kvc, _attn_bias_table(na_rpb[0]))
    return _ffn(x, mod_x[1], *ffn(1), tm_mm, mixer=(att, na_w_out[0].astype(BF16), row(ln1_g[1]), row(ln1_b[1])))
```
